```python
import math
import jax, jax.numpy as jnp
from jax import lax
import numpy as np

D_MODEL = 4096
BATCH = 8
SEQ = 2048
DEPTH = 4
DEC_BATCH = 32
DEC_SEQ = 32
PAST_LEN = 1024

CHUNK = 64
N_Q_HEADS = 16
N_KV_HEADS = 4
HEAD_DIM = 128
GQA_GROUP = N_Q_HEADS // N_KV_HEADS
ATT_WIDTH = N_Q_HEADS * HEAD_DIM
KV_WIDTH = N_KV_HEADS * HEAD_DIM
WINDOW = 128
WIN_CHUNKS = WINDOW // CHUNK
N_BUCKETS = 32
MAX_DISTANCE = 128
M_HEADS = 8
M_QK_DIM = 128
M_V_DIM = 256
M_QK_WIDTH = M_HEADS * M_QK_DIM
M_V_WIDTH = M_HEADS * M_V_DIM
D_FF = 8192
N_EXPERTS = 8
TOP_K = 2
D_FF_EXPERT = 1024
N_DENSE = (DEPTH + 1) // 2
N_MOE = DEPTH // 2
EPS = 1e-6
IN_SIZES = (ATT_WIDTH, KV_WIDTH, KV_WIDTH, M_QK_WIDTH, M_QK_WIDTH, M_V_WIDTH, M_V_WIDTH, M_HEADS, M_HEADS, D_MODEL, D_MODEL)
IN_COLS = sum(IN_SIZES)

kernel_name = 'chunk_causal_swa_mlstm_hybrid_step'

F32 = jnp.float32


def rmsnorm(x, g):
    xf = x.astype(F32)
    y = xf * lax.rsqrt(jnp.mean(xf * xf, axis=-1, keepdims=True) + EPS)
    return (y * g.astype(F32)).astype(x.dtype)


def rel_bucket(rel):
    nb = N_BUCKETS // 2
    max_exact = nb // 2
    ret = jnp.where(rel > 0, nb, 0)
    n = jnp.abs(rel)
    nf = jnp.maximum(n, 1).astype(F32)
    large = max_exact + (jnp.log(nf / max_exact) / math.log(MAX_DISTANCE / max_exact) * (nb - max_exact)).astype(jnp.int32)
    large = jnp.minimum(large, nb - 1)
    return ret + jnp.where(n < max_exact, n, large)


def rel_bias(table, n_q, n_k, key_offset):
    rel = jnp.arange(n_k)[None, :] - key_offset - jnp.arange(n_q)[:, None]
    b = table[rel_bucket(rel)]
    return b.transpose(2, 0, 1).reshape(N_KV_HEADS, GQA_GROUP, n_q, n_k).astype(F32)


def band_attention(q, k, v, bias, mask, sink):
    s = jnp.einsum('bnqkgd,bnskd->bnkgqs', q, k).astype(F32) * HEAD_DIM ** -0.5 + bias
    s = jnp.where(mask[:, None, None], s, -jnp.inf)
    sk = sink.astype(F32)[:, :, None, None]
    m = jnp.maximum(s.max(axis=-1, keepdims=True), sk)
    p = jnp.exp(s - m)
    p = p / (p.sum(axis=-1, keepdims=True) + jnp.exp(sk - m))
    return jnp.einsum('bnkgqs,bnskd->bnqkgd', p.astype(v.dtype), v)


def attn_prompt(qa, ka, va, table, sink):
    B, S = qa.shape[:2]
    NC = S // CHUNK
    LK = (WIN_CHUNKS + 1) * CHUNK
    qc = qa.reshape(B, NC, CHUNK, N_KV_HEADS, GQA_GROUP, HEAD_DIM)

    def band(a):
        ac = a.reshape(B, NC, CHUNK, N_KV_HEADS, HEAD_DIM)
        ap = jnp.pad(ac, ((0, 0), (WIN_CHUNKS, 0), (0, 0), (0, 0), (0, 0)))
        return jnp.concatenate([ap[:, j:j + NC] for j in range(WIN_CHUNKS + 1)], axis=2)

    bias = rel_bias(table, CHUNK, LK, WINDOW)
    key_chunk = jnp.arange(NC)[:, None] - WIN_CHUNKS + jnp.arange(LK)[None, :] // CHUNK
    mask = (key_chunk >= 0)[:, None, :]
    o = band_attention(qc, band(ka), band(va), bias, mask, sink.reshape(N_KV_HEADS, GQA_GROUP))
    return o.reshape(B, S, ATT_WIDTH)


def attn_sample(qa, ka, va, ck, cv, table, sink):
    B, T = qa.shape[:2]
    kk = jnp.concatenate([ck.astype(ka.dtype), ka], axis=1)[:, None]
    vv = jnp.concatenate([cv.astype(va.dtype), va], axis=1)[:, None]
    bias = rel_bias(table, T, WINDOW + T, WINDOW)
    mask = jnp.ones((1, 1, WINDOW + T), dtype=bool)
    o = band_attention(qa[:, None], kk, vv, bias, mask, sink.reshape(N_KV_HEADS, GQA_GROUP))
    return o.reshape(B, T, ATT_WIDTH)


def mlstm_chunk(carry, xs):
    C, n, m = carry
    q, k, v, ig, lf = xs
    L = q.shape[1]
    b = jnp.cumsum(lf, axis=1).transpose(0, 2, 1)
    ih = ig.transpose(0, 2, 1)
    D = b[..., :, None] - b[..., None, :] + ih[..., None, :]
    D = jnp.where(jnp.tril(jnp.ones((L, L), dtype=bool)), D, -jnp.inf)
    inter = m[..., None] + b
    mt = jnp.maximum(inter, D.max(axis=-1))
    P = jnp.exp(D - mt[..., None])
    w_inter = jnp.exp(inter - mt)
    Sqk = jnp.einsum('bthd,bshd->bhts', q, k) * P
    num = jnp.einsum('bhts,bshe->bhte', Sqk, v) + w_inter[..., None] * jnp.einsum('bthd,bhde->bhte', q, C)
    den = Sqk.sum(axis=-1) + w_inter * jnp.einsum('bthd,bhd->bht', q, n)
    h = num / jnp.maximum(jnp.abs(den), jnp.exp(-mt))[..., None]
    m_new = mt[..., -1]
    g = jnp.exp(b[..., -1:] - b + ih - m_new[..., None]).transpose(0, 2, 1)[..., None]
    decay = jnp.exp(m + b[..., -1] - m_new)
    C_new = decay[..., None, None] * C + jnp.einsum('bshd,bshe->bhde', k * g, v)
    n_new = decay[..., None] * n + jnp.sum(k * g, axis=1)
    return (C_new, n_new, m_new), h.transpose(0, 2, 1, 3)


def mlstm_prompt(q, k, v, ig, lf):
    B, S = q.shape[:2]
    NC = S // CHUNK

    def to_chunks(a):
        return a.reshape(B, NC, CHUNK, *a.shape[2:]).swapaxes(0, 1)

    init = (jnp.zeros((B, M_HEADS, M_QK_DIM, M_V_DIM), F32), jnp.zeros((B, M_HEADS, M_QK_DIM), F32), jnp.zeros((B, M_HEADS), F32))
    state, h = lax.scan(mlstm_chunk, init, tuple(to_chunks(a) for a in (q, k, v, ig, lf)))
    return h.swapaxes(0, 1).reshape(B, S, M_HEADS, M_V_DIM), state


def mlstm_out(h, og, gain):
    h = h * lax.rsqrt(jnp.mean(h * h, axis=-1, keepdims=True) + EPS)
    B, T = h.shape[:2]
    h = h.reshape(B, T, M_V_WIDTH) * gain.astype(F32)
    return (jax.nn.sigmoid(og.astype(F32)) * h).astype(og.dtype)


def project_inputs(xn, w_in_l, q_gain, k_gain, b_i, b_f):
    h = xn @ w_in_l
    idx = np.cumsum(IN_SIZES)[:-1].tolist()
    qa, ka, va, qm, km, vm, om, im, fm, ga, gm = jnp.split(h, idx, axis=-1)
    B, T = xn.shape[:2]
    qa = rmsnorm(qa.reshape(B, T, N_KV_HEADS, GQA_GROUP, HEAD_DIM), q_gain)
    ka = rmsnorm(ka.reshape(B, T, N_KV_HEADS, HEAD_DIM), k_gain)
    va = va.reshape(B, T, N_KV_HEADS, HEAD_DIM)
    qm = qm.reshape(B, T, M_HEADS, M_QK_DIM).astype(F32)
    km = km.reshape(B, T, M_HEADS, M_QK_DIM).astype(F32) * M_QK_DIM ** -0.5
    vm = vm.reshape(B, T, M_HEADS, M_V_DIM).astype(F32)
    ig = im.astype(F32) + b_i.astype(F32)
    lf = jax.nn.log_sigmoid(fm.astype(F32) + b_f.astype(F32))
    return qa, ka, va, (qm, km, vm, ig, lf), om, ga, gm


def merge_branches(ao, mo, ga, gm, w_pa, w_pm, w_o):
    u = jax.nn.sigmoid(ga) * (ao @ w_pa) + jax.nn.sigmoid(gm) * (mo @ w_pm)
    return u @ w_o


def swiglu(x, wg, wu, wd):
    return (jax.nn.silu(x @ wg) * (x @ wu)) @ wd


def moe_swiglu(x, w_r, b_r, wg, wu, wd):
    logits = (x @ w_r).astype(F32) + b_r.astype(F32)
    top_v, top_i = lax.top_k(logits, TOP_K)
    w = jax.nn.softmax(top_v, axis=-1)
    gates = jnp.sum(jax.nn.one_hot(top_i, N_EXPERTS, dtype=F32) * w[..., None], axis=-2)
    y = jnp.zeros_like(x)
    for e in range(N_EXPERTS):
        y = y + gates[..., e:e + 1].astype(x.dtype) * swiglu(x, wg[e], wu[e], wd[e])
    return y


def setup_inputs(seed: int = 0) -> dict:
    key = jax.random.key(seed)
    ks = jax.random.split(key, 32)

    def nrm(k, shape, scale):
        return jax.random.normal(k, shape, F32) * scale

    out_scale = (2 * DEPTH) ** -0.5
    return {
        'x_prompt': nrm(ks[0], (BATCH, SEQ, D_MODEL), 1.0),
        'x_sample': nrm(ks[1], (DEC_BATCH, DEC_SEQ, D_MODEL), 1.0),
        'cache_attn_k': nrm(ks[2], (DEC_BATCH, DEPTH, WINDOW, N_KV_HEADS, HEAD_DIM), 1.0),
        'cache_attn_v': nrm(ks[3], (DEC_BATCH, DEPTH, WINDOW, N_KV_HEADS, HEAD_DIM), 1.0),
        'state_mlstm_C': nrm(ks[4], (DEC_BATCH, DEPTH, M_HEADS, M_QK_DIM, M_V_DIM), 0.3),
        'state_mlstm_n': nrm(ks[5], (DEC_BATCH, DEPTH, M_HEADS, M_QK_DIM), 0.3),
        'state_mlstm_m': nrm(ks[6], (DEC_BATCH, DEPTH, M_HEADS), 0.5),
        'rel_bias_table': nrm(ks[7], (N_BUCKETS, N_Q_HEADS), 0.2),
        'norm_mix': 1.0 + nrm(ks[8], (DEPTH, D_MODEL), 0.02),
        'w_in': nrm(ks[9], (DEPTH, D_MODEL, IN_COLS), D_MODEL ** -0.5),
        'q_norm': 1.0 + nrm(ks[10], (DEPTH, HEAD_DIM), 0.02),
        'k_norm': 1.0 + nrm(ks[11], (DEPTH, HEAD_DIM), 0.02),
        'attn_sink': nrm(ks[12], (DEPTH, N_Q_HEADS), 0.5),
        'b_igate': nrm(ks[13], (DEPTH, M_HEADS), 0.1),
        'b_fgate': jnp.linspace(3.0, 6.0, M_HEADS, dtype=F32)[None, :] + nrm(ks[14], (DEPTH, M_HEADS), 0.1),
        'mlstm_norm': 1.0 + nrm(ks[15], (DEPTH, M_V_WIDTH), 0.02),
        'w_proj_attn': nrm(ks[16], (DEPTH, ATT_WIDTH, D_MODEL), ATT_WIDTH ** -0.5),
        'w_proj_mlstm': nrm(ks[17], (DEPTH, M_V_WIDTH, D_MODEL), M_V_WIDTH ** -0.5),
        'w_out': nrm(ks[18], (DEPTH, D_MODEL, D_MODEL), D_MODEL ** -0.5 * out_scale),
        'norm_ffn': 1.0 + nrm(ks[19], (DEPTH, D_MODEL), 0.02),
        'ffn_w_gate': nrm(ks[20], (N_DENSE, D_MODEL, D_FF), D_MODEL ** -0.5),
        'ffn_w_up': nrm(ks[21], (N_DENSE, D_MODEL, D_FF), D_MODEL ** -0.5),
        'ffn_w_down': nrm(ks[22], (N_DENSE, D_FF, D_MODEL), D_FF ** -0.5 * out_scale),
        'w_router': nrm(ks[23], (N_MOE, D_MODEL, N_EXPERTS), D_MODEL ** -0.5),
        'b_router': nrm(ks[24], (N_MOE, N_EXPERTS), 0.01),
        'moe_w_gate': nrm(ks[25], (N_MOE, N_EXPERTS, D_MODEL, D_FF_EXPERT), D_MODEL ** -0.5),
        'moe_w_up': nrm(ks[26], (N_MOE, N_EXPERTS, D_MODEL, D_FF_EXPERT), D_MODEL ** -0.5),
        'moe_w_down': nrm(ks[27], (N_MOE, N_EXPERTS, D_FF_EXPERT, D_MODEL), D_FF_EXPERT ** -0.5 * out_scale),
    }


def reference(x_prompt, x_sample, cache_attn_k, cache_attn_v, state_mlstm_C, state_mlstm_n, state_mlstm_m,
              rel_bias_table, norm_mix, w_in, q_norm, k_norm, attn_sink, b_igate, b_fgate, mlstm_norm,
              w_proj_attn, w_proj_mlstm, w_out, norm_ffn, ffn_w_gate, ffn_w_up, ffn_w_down,
              w_router, b_router, moe_w_gate, moe_w_up, moe_w_down):
    xp, xs = x_prompt, x_sample
    sdt = state_mlstm_C.dtype
    pk, pv, pC, pn, pm = [], [], [], [], []
    sk, sv, sC, sn, sm = [], [], [], [], []
    for l in range(DEPTH):
        mix_w = (w_in[l], q_norm[l], k_norm[l], b_igate[l], b_fgate[l])
        merge_w = (w_proj_attn[l], w_proj_mlstm[l], w_out[l])
        qa, ka, va, mx, om, ga, gm = project_inputs(rmsnorm(xp, norm_mix[l]), *mix_w)
        ao = attn_prompt(qa, ka, va, rel_bias_table, attn_sink[l])
        hm, (C, n, m) = mlstm_prompt(*mx)
        xp = xp + merge_branches(ao, mlstm_out(hm, om, mlstm_norm[l]), ga, gm, *merge_w)
        pk.append(ka[:, -WINDOW:]); pv.append(va[:, -WINDOW:])
        pC.append(C.astype(sdt)); pn.append(n.astype(sdt)); pm.append(m.astype(sdt))
        qa, ka, va, mx, om, ga, gm = project_inputs(rmsnorm(xs, norm_mix[l]), *mix_w)
        ao = attn_sample(qa, ka, va, cache_attn_k[:, l], cache_attn_v[:, l], rel_bias_table, attn_sink[l])
        carry = (state_mlstm_C[:, l].astype(F32), state_mlstm_n[:, l].astype(F32), state_mlstm_m[:, l].astype(F32))
        (C, n, m), hm = mlstm_chunk(carry, mx)
        xs = xs + merge_branches(ao, mlstm_out(hm, om, mlstm_norm[l]), ga, gm, *merge_w)
        sk.append(ka); sv.append(va)
        sC.append(C.astype(sdt)); sn.append(n.astype(sdt)); sm.append(m.astype(sdt))
        j = l // 2
        if l % 2 == 0:
            xp = xp + swiglu(rmsnorm(xp, norm_ffn[l]), ffn_w_gate[j], ffn_w_up[j], ffn_w_down[j])
            xs = xs + swiglu(rmsnorm(xs, norm_ffn[l]), ffn_w_gate[j], ffn_w_up[j], ffn_w_down[j])
        else:
            xp = xp + moe_swiglu(rmsnorm(xp, norm_ffn[l]), w_router[j], b_router[j], moe_w_gate[j], moe_w_up[j], moe_w_down[j])
            xs = xs + moe_swiglu(rmsnorm(xs, norm_ffn[l]), w_router[j], b_router[j], moe_w_gate[j], moe_w_up[j], moe_w_down[j])
    new_k_prompt = jnp.stack(pk, axis=1)
    new_v_prompt = jnp.stack(pv, axis=1)
    new_C_prompt = jnp.stack(pC, axis=1)
    new_n_prompt = jnp.stack(pn, axis=1)
    new_m_prompt = jnp.stack(pm, axis=1)
    new_k_sample = jnp.stack(sk, axis=1)
    new_v_sample = jnp.stack(sv, axis=1)
    new_C_sample = jnp.stack(sC, axis=1)
    new_n_sample = jnp.stack(sn, axis=1)
    new_m_sample = jnp.stack(sm, axis=1)
    return (xp, xs, new_k_prompt, new_v_prompt, new_C_prompt, new_n_prompt, new_m_prompt,
            new_k_sample, new_v_sample, new_C_sample, new_n_sample, new_m_sample)
```

```python
import functools
import math

import jax
import jax.numpy as jnp
from jax import lax
from jax.experimental import pallas as pl
from jax.experimental.pallas import tpu as pltpu

F32 = jnp.float32
BF16 = jnp.bfloat16

CHUNK = 64
WINDOW = 128
N_BUCKETS = 32
MAX_DISTANCE = 128
EPS = 1e-6

V7X_VMEM_LIMIT_BYTES = 56 * 1024 * 1024
LANES = 128


def _pick(dim, prefs):
    for p in prefs:
        if dim % p == 0:
            return p
    raise ValueError(f"no tile in {prefs} divides {dim}")


def _params(*sem):
    return pltpu.CompilerParams(dimension_semantics=sem, vmem_limit_bytes=V7X_VMEM_LIMIT_BYTES)


def _rmsnorm_kernel(x_ref, g_ref, o_ref):
    x = x_ref[...]
    y = x * lax.rsqrt(jnp.mean(x * x, axis=-1, keepdims=True) + EPS)
    o_ref[...] = (y * g_ref[...]).astype(o_ref.dtype)


def rmsnorm(x, g):
    M, D = x.shape
    tm = _pick(M, (512, 256, 128))
    return pl.pallas_call(
        _rmsnorm_kernel,
        grid=(M // tm,),
        in_specs=[pl.BlockSpec((tm, D), lambda i: (i, 0)), pl.BlockSpec((1, D), lambda i: (0, 0))],
        out_specs=pl.BlockSpec((tm, D), lambda i: (i, 0)),
        out_shape=jax.ShapeDtypeStruct((M, D), BF16),
        compiler_params=_params("parallel"),
        name="rmsnorm",
    )(x, g.reshape(1, D))


def _split2(v):
    hi = v.astype(BF16)
    mid = (v - hi.astype(F32)).astype(BF16)
    return hi, mid


def _split3(v):
    hi = v.astype(BF16)
    r = v - hi.astype(F32)
    mid = r.astype(BF16)
    lo = (r - mid.astype(F32)).astype(BF16)
    return hi, mid, lo


def _rmsnorm_router_kernel(x_ref, g_ref, wr_ref, br_ref, o_ref, gates_ref, *, n_experts):
    x = x_ref[...]
    xn = x * lax.rsqrt(jnp.mean(x * x, axis=-1, keepdims=True) + EPS) * g_ref[...]
    o_ref[...] = xn.astype(o_ref.dtype)
    x_hi, x_mid = _split2(xn)
    w_hi, w_mid = wr_ref[0], wr_ref[1]
    logits = (jnp.dot(x_hi, w_hi, preferred_element_type=F32)
              + jnp.dot(x_hi, w_mid, preferred_element_type=F32)
              + jnp.dot(x_mid, w_hi, preferred_element_type=F32)) + br_ref[...]
    lane = lax.broadcasted_iota(jnp.int32, logits.shape, 1)
    neg = -jnp.inf
    logits = jnp.where(lane < n_experts, logits, neg)
    m1 = jnp.max(logits, axis=-1, keepdims=True)
    i1 = jnp.min(jnp.where(logits == m1, lane, LANES), axis=-1, keepdims=True)
    oh1 = lane == i1
    rest = jnp.where(oh1, neg, logits)
    m2 = jnp.max(rest, axis=-1, keepdims=True)
    i2 = jnp.min(jnp.where(rest == m2, lane, LANES), axis=-1, keepdims=True)
    oh2 = lane == i2
    e = jnp.exp(m2 - m1)
    w1 = 1.0 / (1.0 + e)
    w2 = e / (1.0 + e)
    gates_ref[...] = jnp.where(oh1, w1, 0.0) + jnp.where(oh2, w2, 0.0)


def rmsnorm_router(x, g, w_r, b_r):
    M, D = x.shape
    E = w_r.shape[1]
    tm = _pick(M, (512, 256, 128))
    w_pad = jnp.pad(w_r, ((0, 0), (0, LANES - E)))
    w_hi = w_pad.astype(BF16)
    w_mid = (w_pad - w_hi.astype(F32)).astype(BF16)
    w2 = jnp.stack([w_hi, w_mid])
    b_pad = jnp.pad(b_r.astype(F32), (0, LANES - E)).reshape(1, LANES)
    return pl.pallas_call(
        functools.partial(_rmsnorm_router_kernel, n_experts=E),
        grid=(M // tm,),
        in_specs=[pl.BlockSpec((tm, D), lambda i: (i, 0)),
                  pl.BlockSpec((1, D), lambda i: (0, 0)),
                  pl.BlockSpec((2, D, LANES), lambda i: (0, 0, 0)),
                  pl.BlockSpec((1, LANES), lambda i: (0, 0))],
        out_specs=[pl.BlockSpec((tm, D), lambda i: (i, 0)), pl.BlockSpec((tm, LANES), lambda i: (i, 0))],
        out_shape=[jax.ShapeDtypeStruct((M, D), BF16), jax.ShapeDtypeStruct((M, LANES), F32)],
        compiler_params=_params("parallel"),
        name="rmsnorm_router",
    )(x, g.reshape(1, D), w2, b_pad)


def _mm_kernel(x_ref, w_ref, o_ref):
    o_ref[...] = jnp.dot(x_ref[...], w_ref[...], preferred_element_type=F32).astype(o_ref.dtype)


def matmul(x, w, out_dtype=F32):
    M, K = x.shape
    N = w.shape[1]
    tm = _pick(M, (1024, 512, 256, 128))
    tn = _pick(N, (1024, 512, 256, 128))
    return pl.pallas_call(
        _mm_kernel,
        grid=(M // tm, N // tn),
        in_specs=[pl.BlockSpec((tm, K), lambda i, j: (i, 0)), pl.BlockSpec((K, tn), lambda i, j: (0, j))],
        out_specs=pl.BlockSpec((tm, tn), lambda i, j: (i, j)),
        out_shape=jax.ShapeDtypeStruct((M, N), out_dtype),
        compiler_params=_params("parallel", "parallel"),
        name="matmul",
    )(x, w)


def _mm_residual_kernel(x_ref, w_ref, r_ref, o_ref):
    k = pl.program_id(2)
    acc = jnp.dot(x_ref[...], w_ref[...], preferred_element_type=F32)

    @pl.when(k == 0)
    def _():
        o_ref[...] = r_ref[...] + acc

    @pl.when(k != 0)
    def _():
        o_ref[...] += acc


def matmul_residual(x, w, res):
    M, K = x.shape
    N = w.shape[1]
    tm = _pick(M, (1024, 512, 256, 128))
    tn = _pick(N, (1024, 512, 256, 128))
    tk = _pick(K, (2048, 1024, 512))
    return pl.pallas_call(
        _mm_residual_kernel,
        grid=(M // tm, N // tn, K // tk),
        in_specs=[pl.BlockSpec((tm, tk), lambda i, j, k: (i, k)),
                  pl.BlockSpec((tk, tn), lambda i, j, k: (k, j)),
                  pl.BlockSpec((tm, tn), lambda i, j, k: (i, j))],
        out_specs=pl.BlockSpec((tm, tn), lambda i, j, k: (i, j)),
        out_shape=jax.ShapeDtypeStruct((M, N), F32),
        compiler_params=_params("parallel", "parallel", "arbitrary"),
        name="matmul_residual",
    )(x, w, res)


def _merge_kernel(a_ref, m_ref, wa_ref, wm_ref, ga_ref, gm_ref, o_ref):
    pa = jnp.dot(a_ref[...], wa_ref[...], preferred_element_type=F32)
    pm = jnp.dot(m_ref[...], wm_ref[...], preferred_element_type=F32)
    u = jax.nn.sigmoid(ga_ref[...]) * pa + jax.nn.sigmoid(gm_ref[...]) * pm
    o_ref[...] = u.astype(o_ref.dtype)


def merge_gated(ao, mo, w_pa, w_pm, hmain, off_ga, off_gm):
    M, Ka = ao.shape
    Km = mo.shape[1]
    N = w_pa.shape[1]
    tm = _pick(M, (512, 256, 128))
    tn = _pick(math.gcd(N, math.gcd(off_ga, off_gm)), (1024, 512, 256, 128))
    ja, jm = off_ga // tn, off_gm // tn
    return pl.pallas_call(
        _merge_kernel,
        grid=(M // tm, N // tn),
        in_specs=[pl.BlockSpec((tm, Ka), lambda i, j: (i, 0)),
                  pl.BlockSpec((tm, Km), lambda i, j: (i, 0)),
                  pl.BlockSpec((Ka, tn), lambda i, j: (0, j)),
                  pl.BlockSpec((Km, tn), lambda i, j: (0, j)),
                  pl.BlockSpec((tm, tn), lambda i, j: (i, ja + j)),
                  pl.BlockSpec((tm, tn), lambda i, j: (i, jm + j))],
        out_specs=pl.BlockSpec((tm, tn), lambda i, j: (i, j)),
        out_shape=jax.ShapeDtypeStruct((M, N), BF16),
        compiler_params=_params("parallel", "parallel"),
        name="merge_gated",
    )(ao, mo, w_pa, w_pm, hmain, hmain)


def _swiglu_kernel(x_ref, w_ref, *rest, th, tiles_per_expert):
    o_ref = rest[-1]
    acc = jnp.dot(x_ref[...], w_ref[...], preferred_element_type=F32)
    h = jax.nn.silu(acc[:, :th]) * acc[:, th:]
    if tiles_per_expert:
        gates = rest[0][...]
        e = pl.program_id(1) // tiles_per_expert
        lane = lax.broadcasted_iota(jnp.int32, gates.shape, 1)
        h = h * jnp.sum(jnp.where(lane == e, gates, 0.0), axis=-1, keepdims=True)
    o_ref[...] = h.astype(o_ref.dtype)


def swiglu_hidden(xn, w_gu, th, gates=None, tiles_per_expert=0):
    M, K = xn.shape
    N2 = w_gu.shape[1]
    tm = _pick(M, (1024, 512, 256, 128))
    in_specs = [pl.BlockSpec((tm, K), lambda i, j: (i, 0)), pl.BlockSpec((K, 2 * th), lambda i, j: (0, j))]
    args = [xn, w_gu]
    if gates is not None:
        in_specs.append(pl.BlockSpec((tm, LANES), lambda i, j: (i, 0)))
        args.append(gates)
    return pl.pallas_call(
        functools.partial(_swiglu_kernel, th=th, tiles_per_expert=tiles_per_expert),
        grid=(M // tm, N2 // (2 * th)),
        in_specs=in_specs,
        out_specs=pl.BlockSpec((tm, th), lambda i, j: (i, j)),
        out_shape=jax.ShapeDtypeStruct((M, N2 // 2), BF16),
        compiler_params=_params("parallel", "parallel"),
        name="swiglu_hidden",
    )(*args)


def _interleave_gate_up(wg, wu, th):
    K, N = wg.shape
    w = jnp.stack([wg.reshape(K, N // th, th), wu.reshape(K, N // th, th)], axis=2)
    return w.reshape(K, 2 * N).astype(BF16)


def _headnorm(x, gain):
    return x * lax.rsqrt(jnp.mean(x * x, axis=-1, keepdims=True) + EPS) * gain


def _attend(qg, kk, vv, bias, sink_col, valid, scale):
    s = lax.dot_general(qg, kk, (((1,), (1,)), ((), ())), preferred_element_type=F32) * scale + bias
    if valid is not None:
        s = jnp.where(valid, s, -jnp.inf)
    m = jnp.maximum(jnp.max(s, axis=-1, keepdims=True), sink_col)
    p = jnp.exp(s - m)
    denom = jnp.sum(p, axis=-1, keepdims=True) + jnp.exp(sink_col - m)
    p = (p / denom).astype(BF16)
    return jnp.dot(p, vv, preferred_element_type=F32)


def _attn_prompt_kernel(q_ref, k2_ref, k1_ref, k0_ref, v2_ref, v1_ref, v0_ref, bias_ref, sink_ref, qg_ref, kg_ref,
                        o_ref, kn_ref, *, n_kv, group, hd):
    c = pl.program_id(1)
    L = q_ref.shape[0]
    qgain, kgain = qg_ref[...], kg_ref[...]
    col = lax.broadcasted_iota(jnp.int32, (1, 3 * L), 1)
    valid = (c - 2 + col // L) >= 0
    for kv in range(n_kv):
        sl = slice(kv * hd, (kv + 1) * hd)
        k0n = _headnorm(k0_ref[:, sl], kgain)
        kn_ref[:, sl] = k0n
        kk = jnp.concatenate([_headnorm(k2_ref[:, sl], kgain), _headnorm(k1_ref[:, sl], kgain), k0n],
                             axis=0).astype(BF16)
        vv = jnp.concatenate([v2_ref[:, sl], v1_ref[:, sl], v0_ref[:, sl]], axis=0).astype(BF16)
        heads = range(kv * group, (kv + 1) * group)
        qg = jnp.concatenate([_headnorm(q_ref[:, h * hd:(h + 1) * hd], qgain) for h in heads], axis=0).astype(BF16)
        o = _attend(qg, kk, vv, bias_ref[kv], sink_ref[kv], valid, hd ** -0.5)
        for g, h in enumerate(heads):
            o_ref[:, h * hd:(h + 1) * hd] = o[g * L:(g + 1) * L].astype(o_ref.dtype)


def attn_prompt(hmain, bias, sink_col, q_gain, k_gain, *, batch, seq, off_q, off_k, off_v, n_q, n_kv, hd):
    L = CHUNK
    nc = seq // L
    group = n_q // n_kv
    qw, kw = n_q * hd, n_kv * hd
    jq, jk, jv = off_q // qw, off_k // kw, off_v // kw

    def prev(d):
        return lambda b, c: (b * nc + jnp.maximum(c - d, 0), 0)

    def kspec(d, j):
        return pl.BlockSpec((L, kw), lambda b, c: (b * nc + jnp.maximum(c - d, 0), j))

    const3 = lambda b, c: (0, 0, 0)
    const2 = lambda b, c: (0, 0)
    return pl.pallas_call(
        functools.partial(_attn_prompt_kernel, n_kv=n_kv, group=group, hd=hd),
        grid=(batch, nc),
        in_specs=[pl.BlockSpec((L, qw), lambda b, c: (b * nc + c, jq)),
                  kspec(2, jk), kspec(1, jk), kspec(0, jk),
                  kspec(2, jv), kspec(1, jv), kspec(0, jv),
                  pl.BlockSpec((n_kv, group * L, 3 * L), const3),
                  pl.BlockSpec((n_kv, group * L, 1), const3),
                  pl.BlockSpec((1, hd), const2),
                  pl.BlockSpec((1, hd), const2)],
        out_specs=[pl.BlockSpec((L, qw), lambda b, c: (b * nc + c, 0)),
                   pl.BlockSpec((L, kw), lambda b, c: (b * nc + c, 0))],
        out_shape=[jax.ShapeDtypeStruct((batch * seq, qw), BF16), jax.ShapeDtypeStruct((batch * seq, kw), F32)],
        compiler_params=_params("parallel", "arbitrary"),
        name="attn_prompt",
    )(hmain, hmain, hmain, hmain, hmain, hmain, hmain, bias, sink_col, q_gain, k_gain)


def _attn_sample_kernel(q_ref, k_ref, v_ref, ck_ref, cv_ref, bias_ref, sink_ref, qg_ref, kg_ref,
                        o_ref, kn_ref, *, n_kv, group, hd):
    T = q_ref.shape[0]
    qgain, kgain = qg_ref[...], kg_ref[...]
    for kv in range(n_kv):
        sl = slice(kv * hd, (kv + 1) * hd)
        kn = _headnorm(k_ref[:, sl], kgain)
        kn_ref[:, sl] = kn
        kk = jnp.concatenate([ck_ref[:, sl], kn], axis=0).astype(BF16)
        vv = jnp.concatenate([cv_ref[:, sl], v_ref[:, sl]], axis=0).astype(BF16)
        heads = range(kv * group, (kv + 1) * group)
        qg = jnp.concatenate([_headnorm(q_ref[:, h * hd:(h + 1) * hd], qgain) for h in heads], axis=0).astype(BF16)
        o = _attend(qg, kk, vv, bias_ref[kv], sink_ref[kv], None, hd ** -0.5)
        for g, h in enumerate(heads):
            o_ref[:, h * hd:(h + 1) * hd] = o[g * T:(g + 1) * T].astype(o_ref.dtype)


def attn_sample(hmain, cache_k, cache_v, layer, bias, sink_col, q_gain, k_gain, *, row0, batch, T,
                off_q, off_k, off_v, n_q, n_kv, hd):
    group = n_q // n_kv
    qw, kw = n_q * hd, n_kv * hd
    W = cache_k.shape[2]
    jq, jk, jv = off_q // qw, off_k // kw, off_v // kw
    r0 = row0 // T
    const3 = lambda b: (0, 0, 0)
    const2 = lambda b: (0, 0)
    return pl.pallas_call(
        functools.partial(_attn_sample_kernel, n_kv=n_kv, group=group, hd=hd),
        grid=(batch,),
        in_specs=[pl.BlockSpec((T, qw), lambda b: (r0 + b, jq)),
                  pl.BlockSpec((T, kw), lambda b: (r0 + b, jk)),
                  pl.BlockSpec((T, kw), lambda b: (r0 + b, jv)),
                  pl.BlockSpec((None, None, W, kw), lambda b: (b, layer, 0, 0)),
                  pl.BlockSpec((None, None, W, kw), lambda b: (b, layer, 0, 0)),
                  pl.BlockSpec((n_kv, group * T, W + T), const3),
                  pl.BlockSpec((n_kv, group * T, 1), const3),
                  pl.BlockSpec((1, hd), const2),
                  pl.BlockSpec((1, hd), const2)],
        out_specs=[pl.BlockSpec((T, qw), lambda b: (b, 0)), pl.BlockSpec((T, kw), lambda b: (b, 0))],
        out_shape=[jax.ShapeDtypeStruct((batch * T, qw), BF16), jax.ShapeDtypeStruct((batch * T, kw), F32)],
        compiler_params=_params("parallel"),
        name="attn_sample",
    )(hmain, hmain, hmain, cache_k, cache_v, bias, sink_col, q_gain, k_gain)


def _rel_bucket(rel):
    nb = N_BUCKETS // 2
    max_exact = nb // 2
    ret = jnp.where(rel > 0, nb, 0)
    n = jnp.abs(rel)
    nf = jnp.maximum(n, 1).astype(F32)
    large = max_exact + (jnp.log(nf / max_exact) / math.log(MAX_DISTANCE / max_exact) * (nb - max_exact)).astype(jnp.int32)
    large = jnp.minimum(large, nb - 1)
    return ret + jnp.where(n < max_exact, n, large)


def _rel_bias(table, n_q, n_k, key_offset, n_kv, group):
    rel = jnp.arange(n_k)[None, :] - key_offset - jnp.arange(n_q)[:, None]
    b = table[_rel_bucket(rel)]
    return b.transpose(2, 0, 1).reshape(n_kv, group * n_q, n_k).astype(F32)


def _log_sigmoid(x):
    return jnp.minimum(x, 0.0) - jnp.log1p(jnp.exp(-jnp.abs(x)))


def _mlstm_kernel(q_ref, k_ref, v_ref, og_ref, g_ref, bif_ref, gain_ref, *rest, H, DK, DV, has_init):
    if has_init:
        C0_ref, n0_ref, m0_ref, mo_ref, C_ref, n_ref, m_ref = rest
    else:
        mo_ref, C_ref, n_ref, m_ref = rest
    c = pl.program_id(1)
    L = q_ref.shape[0]

    @pl.when(c == 0)
    def _():
        if has_init:
            C_ref[...] = C0_ref[...]
            n_ref[...] = n0_ref[...]
            m_ref[...] = m0_ref[...]
        else:
            C_ref[...] = jnp.zeros_like(C_ref)
            n_ref[...] = jnp.zeros_like(n_ref)
            m_ref[...] = jnp.zeros_like(m_ref)

    pre = g_ref[...] + bif_ref[...]
    lane = lax.broadcasted_iota(jnp.int32, pre.shape, 1)
    vals = jnp.where(lane < H, pre, _log_sigmoid(pre))
    row = lax.broadcasted_iota(jnp.int32, (L, L), 0)
    coli = lax.broadcasted_iota(jnp.int32, (L, L), 1)
    causal = row >= coli
    tril = causal.astype(BF16)
    cs = sum(jnp.dot(tril, part, preferred_element_type=F32) for part in _split3(vals))
    cols = jnp.where(lane < H, vals, cs)
    eye = (lax.broadcasted_iota(jnp.int32, (2 * H, LANES), 0)
           == lax.broadcasted_iota(jnp.int32, (2 * H, LANES), 1)).astype(BF16)
    nt = (((1,), (1,)), ((), ()))
    rows = sum(lax.dot_general(eye, part, nt, preferred_element_type=F32) for part in _split3(cols))

    for h in range(H):
        ig_c, b_c = cols[:, h:h + 1], cols[:, H + h:H + h + 1]
        ig_r, b_r = rows[h:h + 1, :], rows[H + h:H + h + 1, :]
        m_prev = m_ref[0, h:h + 1, 0:1]
        n_prev = n_ref[0, h:h + 1, :]
        C_prev = C_ref[0, h]
        D = jnp.where(causal, b_c - b_r + ig_r, -jnp.inf)
        inter = m_prev + b_c
        mt = jnp.maximum(inter, jnp.max(D, axis=-1, keepdims=True))
        P = jnp.exp(D - mt)
        w_inter = jnp.exp(inter - mt)
        q = q_ref[:, h * DK:(h + 1) * DK]
        k = k_ref[:, h * DK:(h + 1) * DK] * DK ** -0.5
        vb = v_ref[:, h * DV:(h + 1) * DV].astype(BF16)
        qb = q.astype(BF16)
        S = lax.dot_general(qb, k.astype(BF16), nt, preferred_element_type=F32) * P
        num = (jnp.dot(S.astype(BF16), vb, preferred_element_type=F32)
               + w_inter * jnp.dot(qb, C_prev.astype(BF16), preferred_element_type=F32))
        den = jnp.sum(S, axis=-1, keepdims=True) + w_inter * jnp.sum(q * n_prev, axis=-1, keepdims=True)
        hh = num / jnp.maximum(jnp.abs(den), jnp.exp(-mt))
        m_new = mt[L - 1:L, :]
        b_last = b_c[L - 1:L, :]
        kg = k * jnp.exp(b_last - b_c + ig_c - m_new)
        decay = jnp.exp(m_prev + b_last - m_new)
        C_ref[0, h] = decay * C_prev + lax.dot_general(kg.astype(BF16), vb, (((0,), (0,)), ((), ())),
                                                       preferred_element_type=F32)
        n_ref[0, h:h + 1, :] = decay * n_prev + jnp.sum(kg, axis=0, keepdims=True)
        m_ref[0, h:h + 1, :] = jnp.broadcast_to(m_new, (1, LANES))
        hn = hh * lax.rsqrt(jnp.mean(hh * hh, axis=-1, keepdims=True) + EPS) * gain_ref[:, h * DV:(h + 1) * DV]
        mo_ref[:, h * DV:(h + 1) * DV] = (jax.nn.sigmoid(og_ref[:, h * DV:(h + 1) * DV]) * hn).astype(mo_ref.dtype)


def mlstm(hmain, hif, bif, gain, init, *, row0, batch, n_chunks, L, off_q, off_k, off_v, off_o, H, DK, DV):
    qw, vw = H * DK, H * DV
    jq, jk, jv, jo = off_q // qw, off_k // qw, off_v // vw, off_o // vw
    r0 = row0 // L
    rowmap = lambda j: (lambda b, c: (r0 + b * n_chunks + c, j))
    const2 = lambda b, c: (0, 0)
    in_specs = [pl.BlockSpec((L, qw), rowmap(jq)), pl.BlockSpec((L, qw), rowmap(jk)),
                pl.BlockSpec((L, vw), rowmap(jv)), pl.BlockSpec((L, vw), rowmap(jo)),
                pl.BlockSpec((L, LANES), rowmap(0)),
                pl.BlockSpec((1, LANES), const2), pl.BlockSpec((1, vw), const2)]
    args = [hmain, hmain, hmain, hmain, hif, bif, gain]
    state_specs = [pl.BlockSpec((1, H, DK, DV), lambda b, c: (b, 0, 0, 0)),
                   pl.BlockSpec((1, H, DK), lambda b, c: (b, 0, 0)),
                   pl.BlockSpec((1, H, LANES), lambda b, c: (b, 0, 0))]
    if init is not None:
        in_specs += state_specs
        args += list(init)
    return pl.pallas_call(
        functools.partial(_mlstm_kernel, H=H, DK=DK, DV=DV, has_init=init is not None),
        grid=(batch, n_chunks),
        in_specs=in_specs,
        out_specs=[pl.BlockSpec((L, vw), lambda b, c: (b * n_chunks + c, 0))] + state_specs,
        out_shape=[jax.ShapeDtypeStruct((batch * n_chunks * L, vw), BF16),
                   jax.ShapeDtypeStruct((batch, H, DK, DV), F32),
                   jax.ShapeDtypeStruct((batch, H, DK), F32),
                   jax.ShapeDtypeStruct((batch, H, LANES), F32)],
        compiler_params=_params("parallel", "arbitrary"),
        name="mlstm",
    )(*args)


def kernel(x_prompt, x_sample, cache_attn_k, cache_attn_v, state_mlstm_C, state_mlstm_n, state_mlstm_m, rel_bias_table, norm_mix, w_in, q_norm, k_norm, attn_sink, b_igate, b_fgate, mlstm_norm, w_proj_attn, w_proj_mlstm, w_out, norm_ffn, ffn_w_gate, ffn_w_up, ffn_w_down, w_router, b_router, moe_w_gate, moe_w_up, moe_w_down):
    B, S, D = x_prompt.shape
    DB, T, _ = x_sample.shape
    depth = w_in.shape[0]
    _, _, W, n_kv, hd = cache_attn_k.shape
    n_q = attn_sink.shape[1]
    group = n_q // n_kv
    _, _, H, DK, DV = state_mlstm_C.shape
    E, _, F_e = moe_w_gate.shape[1:]
    sdt = state_mlstm_C.dtype
    assert W == WINDOW and S % CHUNK == 0 and 2 * H <= LANES
    Mp, Ms = B * S, DB * T
    att_w, kv_w, mqk_w, mv_w = n_q * hd, n_kv * hd, H * DK, H * DV

    sizes = (att_w, kv_w, kv_w, mqk_w, mqk_w, mv_w, mv_w, H, H, D, D)
    names = ("qa", "ka", "va", "qm", "km", "vm", "om", "im", "fm", "ga", "gm")
    src, o = {}, 0
    for nme, sz in zip(names, sizes):
        src[nme] = (o, sz)
        o += sz
    assert o == w_in.shape[2]
    order = ("qa", "vm", "om", "qm", "km", "ka", "va", "ga", "gm")
    off, o = {}, 0
    for nme in order:
        off[nme] = o
        o += src[nme][1]
    w_main = jnp.concatenate([w_in[:, :, src[n_][0]:src[n_][0] + src[n_][1]] for n_ in order], axis=2).astype(BF16)
    w_if = jnp.concatenate([w_in[:, :, src["im"][0]:src["fm"][0] + H],
                            jnp.zeros((depth, D, LANES - 2 * H), w_in.dtype)], axis=2).astype(BF16)
    bif = jnp.concatenate([b_igate.astype(F32), b_fgate.astype(F32),
                           jnp.zeros((depth, LANES - 2 * H), F32)], axis=1).reshape(depth, 1, LANES)

    w_pa = w_proj_attn.astype(BF16)
    w_pm = w_proj_mlstm.astype(BF16)
    w_o = w_out.astype(BF16)
    th_d = _pick(ffn_w_gate.shape[2], (512, 256, 128))
    th_e = _pick(F_e, (512, 256, 128))

    bias_p = _rel_bias(rel_bias_table, CHUNK, 3 * CHUNK, WINDOW, n_kv, group)
    bias_s = _rel_bias(rel_bias_table, T, WINDOW + T, WINDOW, n_kv, group)
    ck = cache_attn_k.reshape(DB, depth, W, kv_w)
    cv = cache_attn_v.reshape(DB, depth, W, kv_w)
    m0_lanes = jnp.broadcast_to(state_mlstm_m.astype(F32)[..., None], (DB, depth, H, LANES))

    x = jnp.concatenate([x_prompt.reshape(Mp, D), x_sample.reshape(Ms, D)], axis=0)
    outs = {k_: [] for k_ in ("pk", "pv", "pC", "pn", "pm", "sk", "sv", "sC", "sn", "sm")}
    heads = dict(n_q=n_q, n_kv=n_kv, hd=hd)
    for l in range(depth):
        xn = rmsnorm(x, norm_mix[l])
        hmain = matmul(xn, w_main[l])
        hif = matmul(xn, w_if[l])
        sink = attn_sink[l].astype(F32)
        qg, kg = q_norm[l].astype(F32).reshape(1, hd), k_norm[l].astype(F32).reshape(1, hd)
        ao_p, kn_p = attn_prompt(hmain, bias_p, jnp.repeat(sink, CHUNK).reshape(n_kv, group * CHUNK, 1), qg, kg,
                                 batch=B, seq=S, off_q=off["qa"], off_k=off["ka"], off_v=off["va"], **heads)
        ao_s, kn_s = attn_sample(hmain, ck, cv, l, bias_s, jnp.repeat(sink, T).reshape(n_kv, group * T, 1), qg, kg,
                                 row0=Mp, batch=DB, T=T, off_q=off["qa"], off_k=off["ka"], off_v=off["va"], **heads)
        gain = mlstm_norm[l].astype(F32).reshape(1, mv_w)
        mdims = dict(off_q=off["qm"], off_k=off["km"], off_v=off["vm"], off_o=off["om"], H=H, DK=DK, DV=DV)
        mo_p, Cp, np_, mp = mlstm(hmain, hif, bif[l], gain, None, row0=0, batch=B, n_chunks=S // CHUNK, L=CHUNK, **mdims)
        init = (state_mlstm_C[:, l].astype(F32), state_mlstm_n[:, l].astype(F32), m0_lanes[:, l])
        mo_s, Cs, ns, ms = mlstm(hmain, hif, bif[l], gain, init, row0=Mp, batch=DB, n_chunks=1, L=T, **mdims)
        u = merge_gated(jnp.concatenate([ao_p, ao_s], axis=0), jnp.concatenate([mo_p, mo_s], axis=0),
                        w_pa[l], w_pm[l], hmain, off["ga"], off["gm"])
        x = matmul_residual(u, w_o[l], x)

        va = hmain[:, off["va"]:off["va"] + kv_w]
        outs["pk"].append(kn_p.reshape(B, S, n_kv, hd)[:, S - WINDOW:])
        outs["pv"].append(va[:Mp].reshape(B, S, n_kv, hd)[:, S - WINDOW:])
        outs["pC"].append(Cp.astype(sdt)); outs["pn"].append(np_.astype(sdt)); outs["pm"].append(mp[:, :, 0].astype(sdt))
        outs["sk"].append(kn_s.reshape(DB, T, n_kv, hd))
        outs["sv"].append(va[Mp:].reshape(DB, T, n_kv, hd))
        outs["sC"].append(Cs.astype(sdt)); outs["sn"].append(ns.astype(sdt)); outs["sm"].append(ms[:, :, 0].astype(sdt))

        j = l // 2
        if l % 2 == 0:
            xn = rmsnorm(x, norm_ffn[l])
            hid = swiglu_hidden(xn, _interleave_gate_up(ffn_w_gate[j], ffn_w_up[j], th_d), th_d)
            x = matmul_residual(hid, ffn_w_down[j].astype(BF16), x)
        else:
            xn, gates = rmsnorm_router(x, norm_ffn[l], w_router[j], b_router[j])
            wg = moe_w_gate[j].transpose(1, 0, 2).reshape(D, E * F_e)
            wu = moe_w_up[j].transpose(1, 0, 2).reshape(D, E * F_e)
            hid = swiglu_hidden(xn, _interleave_gate_up(wg, wu, th_e), th_e, gates=gates, tiles_per_expert=F_e // th_e)
            x = matmul_residual(hid, moe_w_down[j].reshape(E * F_e, D).astype(BF16), x)

    st = lambda k_: jnp.stack(outs[k_], axis=1)
    return (x[:Mp].reshape(B, S, D), x[Mp:].reshape(DB, T, D),
            st("pk"), st("pv"), st("pC"), st("pn"), st("pm"),
            st("sk"), st("sv"), st("sC"), st("sn"), st("sm"))
```
